```python
import math
import jax, jax.numpy as jnp
from jax import lax
import numpy as np

D_MODEL = 4096
BATCH = 2
SEQ = 4096
DEPTH = 2

F32 = jnp.float32
EPS = 1e-6
HEAD_DIM = 128
MIX_WIDTH = D_MODEL
N_GROUPS = 4
GROUP_WIDTH = MIX_WIDTH // N_GROUPS
GROUP_HEADS = GROUP_WIDTH // HEAD_DIM

RET_HEADS = GROUP_HEADS
RET_CHUNK = 128
ROPE_BASE = 10000.0
MOBA_HEADS = GROUP_HEADS
MOBA_BLOCK = 256
MOBA_TOPK = 3
MOBA_QCHUNK = 32
LRU_WIDTH = GROUP_WIDTH
LRU_BLOCKS = GROUP_HEADS
LRU_BLOCK_W = LRU_WIDTH // LRU_BLOCKS
CONV_WIDTH = 4
LRU_C = 8.0
NSA_HEADS = GROUP_HEADS
NSA_KV_HEADS = 2
NSA_KV_WIDTH = NSA_KV_HEADS * HEAD_DIM
NSA_BRANCHES = 3
CMP_LEN = 32
CMP_STRIDE = 16
CMP_HIDDEN = 256
SEL_BLOCK = 64
SEL_TOPN = 16
WIN = 512
WIN_QBLOCK = 128
NSA_QCHUNK = 64
D_FF = ((8 * D_MODEL + 3 * 256 - 1) // (3 * 256)) * 256

IN_WIDTHS = ((GROUP_WIDTH,) * 4
             + (GROUP_WIDTH,) * 3
             + (LRU_WIDTH,) * 2
             + (GROUP_WIDTH,)
             + (NSA_KV_WIDTH,) * 6
             + (NSA_HEADS * NSA_BRANCHES,))
IN_WIDTH = sum(IN_WIDTHS)
SPLIT_POINTS = tuple(sum(IN_WIDTHS[:i]) for i in range(1, len(IN_WIDTHS)))

kernel_name = 'hybrid_parallel_heads_trunk'


def rmsnorm(x, w):
    xf = x.astype(F32)
    y = xf * lax.rsqrt(jnp.mean(xf * xf, axis=-1, keepdims=True) + EPS) * w.astype(F32)
    return y.astype(x.dtype)


def masked_softmax(scores, mask):
    s = jnp.where(mask, scores.astype(F32), -jnp.inf)
    m = jnp.max(s, axis=-1, keepdims=True)
    m = jnp.where(jnp.isfinite(m), m, 0.0)
    e = jnp.where(mask, jnp.exp(s - m), 0.0)
    return e / jnp.maximum(jnp.sum(e, axis=-1, keepdims=True), 1e-30)


def to_chunks(t, axis, size):
    n = t.shape[axis] // size
    t = t.reshape(t.shape[:axis] + (n, size) + t.shape[axis + 1:])
    return jnp.moveaxis(t, axis, 0)


def from_chunks(t, axis):
    t = jnp.moveaxis(t, 0, axis)
    return t.reshape(t.shape[:axis] + (t.shape[axis] * t.shape[axis + 1],) + t.shape[axis + 2:])


def rope(x, pos):
    half = x.shape[-1] // 2
    inv = ROPE_BASE ** (-jnp.arange(half, dtype=F32) / half)
    ang = pos.astype(F32)[:, None] * inv[None, :]
    cos = jnp.cos(ang)[None, :, None, :]
    sin = jnp.sin(ang)[None, :, None, :]
    x1, x2 = x[..., :half], x[..., half:]
    return jnp.concatenate([x1 * cos - x2 * sin, x1 * sin + x2 * cos], axis=-1)


def retention(q, k, v, g, gain):
    B_, S_, H, d = q.shape
    pos = jnp.arange(S_)
    q = rope(q.astype(F32), pos)
    k = rope(k.astype(F32), pos) * (d ** -0.5)
    v = v.astype(F32)
    C = RET_CHUNK
    N = S_ // C
    log_gamma = jnp.log1p(-jnp.exp2(-5.0 - jnp.arange(H, dtype=F32)))
    qc = q.reshape(B_, N, C, H, d)
    kc = k.reshape(B_, N, C, H, d)
    vc = v.reshape(B_, N, C, H, d)
    pc = jnp.arange(C, dtype=F32)
    diff = pc[:, None] - pc[None, :]
    decay_intra = jnp.where(diff >= 0, jnp.exp(log_gamma[:, None, None] * jnp.maximum(diff, 0.0)), 0.0)
    scores = jnp.einsum('bnihd,bnjhd->bnhij', qc, kc) * decay_intra
    y_intra = jnp.einsum('bnhij,bnjhd->bnihd', scores, vc)
    decay_to_end = jnp.exp(log_gamma[:, None] * (C - 1.0 - pc)[None, :])
    kv_chunk = jnp.einsum('bnjhd,hj,bnjhe->bnhde', kc, decay_to_end, vc)
    decay_chunk = jnp.exp(log_gamma * C)[None, :, None, None]

    def step(state, kv_n):
        return decay_chunk * state + kv_n, state

    _, state_prev = lax.scan(step, jnp.zeros((B_, H, d, d), F32), jnp.moveaxis(kv_chunk, 1, 0))
    state_prev = jnp.moveaxis(state_prev, 0, 1)
    decay_from_start = jnp.exp(log_gamma[:, None] * (pc + 1.0)[None, :])
    y_cross = jnp.einsum('bnihd,bnhde,hi->bnihe', qc, state_prev, decay_from_start)
    y = (y_intra + y_cross).reshape(B_, S_, H, d)
    y = y - jnp.mean(y, axis=-1, keepdims=True)
    y = y * lax.rsqrt(jnp.mean(y * y, axis=-1, keepdims=True) + EPS)
    y = y.reshape(B_, S_, H * d) * gain.astype(F32)
    return (jax.nn.silu(g.astype(F32)) * y).astype(g.dtype)


def moba_attention(q, k, v):
    B_, S_, H, d = q.shape
    scale = d ** -0.5
    BLK = MOBA_BLOCK
    QC = MOBA_QCHUNK
    NB = -(-S_ // BLK)
    pad = NB * BLK - S_
    qt = q.transpose(0, 2, 1, 3)
    kp = jnp.pad(k.transpose(0, 2, 1, 3), ((0, 0), (0, 0), (0, pad), (0, 0)))
    vp = jnp.pad(v.transpose(0, 2, 1, 3), ((0, 0), (0, 0), (0, pad), (0, 0)))
    kb = kp.reshape(B_, H, NB, BLK, d)
    vb = vp.reshape(B_, H, NB, BLK, d)
    k_mean = jnp.mean(kb.astype(F32), axis=3)
    qblk = jnp.arange(S_) // BLK
    gate = jnp.einsum('bhsd,bhnd->bhsn', qt.astype(F32), k_mean)
    past = jnp.arange(NB)[None, :] < qblk[:, None]
    gate = jnp.where(past, gate, -jnp.inf)
    topk = min(MOBA_TOPK, NB)
    _, sel = lax.top_k(gate, topk)
    sel_valid = sel < qblk[None, None, :, None]
    bi = jnp.arange(B_)[:, None, None, None]
    hi = jnp.arange(H)[None, :, None, None]
    M = topk * BLK

    def one_chunk(args):
        c, q_c, sel_c, val_c = args
        t_c = c * QC + jnp.arange(QC)
        k_sel = kb[bi, hi, sel_c].reshape(B_, H, QC, M, d)
        v_sel = vb[bi, hi, sel_c].reshape(B_, H, QC, M, d)
        own = (c * QC) // BLK * BLK
        k_own = lax.dynamic_slice_in_dim(kp, own, BLK, axis=2)
        v_own = lax.dynamic_slice_in_dim(vp, own, BLK, axis=2)
        s_sel = jnp.einsum('bhqd,bhqmd->bhqm', q_c, k_sel)
        s_own = jnp.einsum('bhqd,bhld->bhql', q_c, k_own)
        m_sel = jnp.broadcast_to(val_c[..., None], (B_, H, QC, topk, BLK)).reshape(B_, H, QC, M)
        m_own = jnp.broadcast_to(((own + jnp.arange(BLK))[None, :] <= t_c[:, None])[None, None], (B_, H, QC, BLK))
        p = masked_softmax(jnp.concatenate([s_sel, s_own], axis=-1) * scale,
                           jnp.concatenate([m_sel, m_own], axis=-1)).astype(v.dtype)
        return (jnp.einsum('bhqm,bhqmd->bhqd', p[..., :M], v_sel)
                + jnp.einsum('bhql,bhld->bhqd', p[..., M:], v_own))

    n_ch = S_ // QC
    out = lax.map(one_chunk, (jnp.arange(n_ch), to_chunks(qt, 2, QC), to_chunks(sel, 2, QC), to_chunks(sel_valid, 2, QC)))
    out = from_chunks(out, 2)
    return out.transpose(0, 2, 1, 3).reshape(B_, S_, H * d)


def rg_lru_block(xb, gb, conv_w, conv_b, wa, ba, wx, bx, lam):
    B_, S_, R = xb.shape
    xp = jnp.pad(xb, ((0, 0), (CONV_WIDTH - 1, 0), (0, 0)))
    xc = conv_b
    for tap in range(CONV_WIDTH):
        xc = xc + xp[:, tap:tap + S_] * conv_w[tap]
    xg = xc.reshape(B_, S_, LRU_BLOCKS, LRU_BLOCK_W)
    r = jax.nn.sigmoid((jnp.einsum('bsgi,gij->bsgj', xg, wa).reshape(B_, S_, R) + ba).astype(F32))
    i = jax.nn.sigmoid((jnp.einsum('bsgi,gij->bsgj', xg, wx).reshape(B_, S_, R) + bx).astype(F32))
    log_a = -LRU_C * r * jax.nn.softplus(-lam.astype(F32))
    a = jnp.exp(log_a)
    u = jnp.sqrt(jnp.maximum(-jnp.expm1(2.0 * log_a), 0.0)) * (i * xc.astype(F32))
    _, h = lax.associative_scan(lambda e1, e2: (e1[0] * e2[0], e2[0] * e1[1] + e2[1]), (a, u), axis=1)
    return (h * jax.nn.gelu(gb.astype(F32))).astype(xb.dtype)


def nsa_attention(q, k_cmp, v_cmp, k_sel, v_sel, k_win, v_win, gate_logits,
                  pos_k, w1_k, w2_k, pos_v, w1_v, w2_v):
    B_, S_, H, d = q.shape
    KVH = k_cmp.shape[2]
    G = H // KVH
    scale = d ** -0.5
    t = jnp.arange(S_)
    qg = q.reshape(B_, S_, KVH, G, d).transpose(0, 2, 3, 1, 4)

    NC = (S_ - CMP_LEN) // CMP_STRIDE + 1
    starts = jnp.arange(NC) * CMP_STRIDE
    cidx = starts[:, None] + jnp.arange(CMP_LEN)[None, :]

    def compress(x, pos, w1, w2):
        blk = x[:, cidx] + pos[None, None, :, None, :]
        flat = blk.transpose(0, 3, 1, 2, 4).reshape(B_, KVH, NC, CMP_LEN * d)
        return jax.nn.gelu(flat @ w1) @ w2

    kc = compress(k_cmp, pos_k, w1_k, w2_k)
    vc = compress(v_cmp, pos_v, w1_v, w2_v)
    s_cmp = jnp.einsum('bkgsd,bknd->bkgsn', qg, kc) * scale
    m_cmp = (starts + CMP_LEN - 1)[None, :] <= t[:, None]
    p_cmp = masked_softmax(s_cmp, m_cmp)
    o_cmp = jnp.einsum('bkgsn,bknd->bkgsd', p_cmp.astype(vc.dtype), vc).astype(F32)

    NSEL = S_ // SEL_BLOCK
    sel_start = jnp.arange(NSEL) * SEL_BLOCK
    overlap = ((starts[:, None] < sel_start[None, :] + SEL_BLOCK)
               & (starts[:, None] + CMP_LEN > sel_start[None, :])).astype(F32)
    imp = jnp.einsum('bkgsn,nj->bksj', p_cmp, overlap)
    qsb = t // SEL_BLOCK
    j = jnp.arange(NSEL)[None, :]
    forced = (j == 0) | (j == qsb[:, None]) | (j == qsb[:, None] - 1)
    allowed = j <= qsb[:, None]
    imp = jnp.where(forced, jnp.inf, jnp.where(allowed, imp, -jnp.inf))
    topn = min(SEL_TOPN, NSEL)
    _, sel = lax.top_k(imp, topn)
    kb = k_sel.transpose(0, 2, 1, 3).reshape(B_, KVH, NSEL, SEL_BLOCK, d)
    vb = v_sel.transpose(0, 2, 1, 3).reshape(B_, KVH, NSEL, SEL_BLOCK, d)
    bi = jnp.arange(B_)[:, None, None, None]
    hi = jnp.arange(KVH)[None, :, None, None]
    QC = NSA_QCHUNK
    M = topn * SEL_BLOCK

    def sel_chunk(args):
        c, q_c, sel_c = args
        t_c = c * QC + jnp.arange(QC)
        kk = kb[bi, hi, sel_c].reshape(B_, KVH, QC, M, d)
        vv = vb[bi, hi, sel_c].reshape(B_, KVH, QC, M, d)
        s = jnp.einsum('bkgqd,bkqmd->bkgqm', q_c, kk) * scale
        kpos = (sel_c[..., None] * SEL_BLOCK + jnp.arange(SEL_BLOCK)).reshape(B_, KVH, QC, M)
        mask = (kpos <= t_c[None, None, :, None])[:, :, None]
        p = masked_softmax(s, mask).astype(vv.dtype)
        return jnp.einsum('bkgqm,bkqmd->bkgqd', p, vv)

    n_ch = S_ // QC
    o_sel = lax.map(sel_chunk, (jnp.arange(n_ch), to_chunks(qg, 3, QC), to_chunks(sel, 2, QC)))
    o_sel = from_chunks(o_sel, 3).astype(F32)

    NQB = S_ // WIN_QBLOCK
    NPB = WIN // WIN_QBLOCK

    def window_blocks(x):
        xp = jnp.pad(x.transpose(0, 2, 1, 3), ((0, 0), (0, 0), (WIN, 0), (0, 0)))
        xp = xp.reshape(B_, KVH, NPB + NQB, WIN_QBLOCK, d)
        return jnp.concatenate([xp[:, :, o:o + NQB] for o in range(NPB + 1)], axis=3)

    kw = window_blocks(k_win)
    vw = window_blocks(v_win)
    qw = qg.reshape(B_, KVH, G, NQB, WIN_QBLOCK, d)
    s_win = jnp.einsum('bkgnqd,bknmd->bkgnqm', qw, kw) * scale
    tq = jnp.arange(S_).reshape(NQB, WIN_QBLOCK)
    kpos = (jnp.arange(NQB) * WIN_QBLOCK - WIN)[:, None] + jnp.arange((NPB + 1) * WIN_QBLOCK)[None, :]
    dist = tq[:, :, None] - kpos[:, None, :]
    m_win = (kpos[:, None, :] >= 0) & (dist >= 0) & (dist < WIN)
    p_win = masked_softmax(s_win, m_win).astype(vw.dtype)
    o_win = jnp.einsum('bkgnqm,bknmd->bkgnqd', p_win, vw).reshape(B_, KVH, G, S_, d).astype(F32)

    gates = jax.nn.sigmoid(gate_logits.astype(F32)).reshape(B_, S_, KVH, G, NSA_BRANCHES).transpose(0, 2, 3, 1, 4)
    o = gates[..., 0:1] * o_cmp + gates[..., 1:2] * o_sel + gates[..., 2:3] * o_win
    return o.transpose(0, 3, 1, 2, 4).reshape(B_, S_, H * d).astype(q.dtype)


def split_heads(t, n_heads):
    return t.reshape(t.shape[0], t.shape[1], n_heads, HEAD_DIM)


def hybrid_layer(x, norm_mix, w_in, w_out, ret_norm, lru_conv_w, lru_conv_b, lru_wa, lru_ba, lru_wx, lru_bx,
                 lru_lambda, cmp_pos_k, cmp_w1_k, cmp_w2_k, cmp_pos_v, cmp_w1_v, cmp_w2_v,
                 norm_ffn, w_gate, w_up, w_down):
    B_, S_, _ = x.shape
    h = rmsnorm(x, norm_mix)
    proj = jnp.einsum('bsd,de->bse', h, w_in)
    (rq, rk, rv, rg, mq, mk, mv, lx, lg, nq, nkc, nvc, nks, nvs, nkw, nvw, ngate) = jnp.split(proj, SPLIT_POINTS, axis=-1)
    y_ret = retention(split_heads(rq, RET_HEADS), split_heads(rk, RET_HEADS), split_heads(rv, RET_HEADS), rg, ret_norm)
    y_moba = moba_attention(split_heads(mq, MOBA_HEADS), split_heads(mk, MOBA_HEADS), split_heads(mv, MOBA_HEADS))
    y_lru = rg_lru_block(lx, lg, lru_conv_w, lru_conv_b, lru_wa, lru_ba, lru_wx, lru_bx, lru_lambda)
    y_nsa = nsa_attention(split_heads(nq, NSA_HEADS),
                          split_heads(nkc, NSA_KV_HEADS), split_heads(nvc, NSA_KV_HEADS),
                          split_heads(nks, NSA_KV_HEADS), split_heads(nvs, NSA_KV_HEADS),
                          split_heads(nkw, NSA_KV_HEADS), split_heads(nvw, NSA_KV_HEADS),
                          ngate.reshape(B_, S_, NSA_HEADS, NSA_BRANCHES),
                          cmp_pos_k, cmp_w1_k, cmp_w2_k, cmp_pos_v, cmp_w1_v, cmp_w2_v)
    y = jnp.concatenate([y_ret, y_moba, y_lru, y_nsa], axis=-1)
    x = x + jnp.einsum('bse,ed->bsd', y, w_out)
    h = rmsnorm(x, norm_ffn)
    u = jax.nn.silu(h @ w_gate) * (h @ w_up)
    return x + u @ w_down


def setup_inputs(seed: int = 0) -> dict:
    key = jax.random.key(seed)
    ks = jax.random.split(key, 24)

    def nrm(k, shape, scale):
        return jax.random.normal(k, shape, F32) * scale

    D = D_MODEL
    a_c = jax.random.uniform(ks[11], (DEPTH, LRU_WIDTH), F32, 0.9, 0.999)
    a = a_c ** (1.0 / LRU_C)
    return {
        'x': nrm(ks[0], (BATCH, SEQ, D), 1.0),
        'norm_mix': 1.0 + nrm(ks[1], (DEPTH, D), 0.02),
        'w_in': nrm(ks[2], (DEPTH, D, IN_WIDTH), D ** -0.5),
        'w_out': nrm(ks[3], (DEPTH, MIX_WIDTH, D), MIX_WIDTH ** -0.5),
        'ret_norm': 1.0 + nrm(ks[4], (DEPTH, GROUP_WIDTH), 0.02),
        'lru_conv_w': nrm(ks[5], (DEPTH, CONV_WIDTH, LRU_WIDTH), CONV_WIDTH ** -0.5),
        'lru_conv_b': nrm(ks[6], (DEPTH, LRU_WIDTH), 0.01),
        'lru_wa': nrm(ks[7], (DEPTH, LRU_BLOCKS, LRU_BLOCK_W, LRU_BLOCK_W), LRU_BLOCK_W ** -0.5),
        'lru_ba': nrm(ks[8], (DEPTH, LRU_WIDTH), 0.01),
        'lru_wx': nrm(ks[9], (DEPTH, LRU_BLOCKS, LRU_BLOCK_W, LRU_BLOCK_W), LRU_BLOCK_W ** -0.5),
        'lru_bx': nrm(ks[10], (DEPTH, LRU_WIDTH), 0.01),
        'lru_lambda': jnp.log(a) - jnp.log1p(-a),
        'cmp_pos_k': nrm(ks[12], (DEPTH, CMP_LEN, HEAD_DIM), 0.02),
        'cmp_w1_k': nrm(ks[13], (DEPTH, CMP_LEN * HEAD_DIM, CMP_HIDDEN), (CMP_LEN * HEAD_DIM) ** -0.5),
        'cmp_w2_k': nrm(ks[14], (DEPTH, CMP_HIDDEN, HEAD_DIM), CMP_HIDDEN ** -0.5),
        'cmp_pos_v': nrm(ks[15], (DEPTH, CMP_LEN, HEAD_DIM), 0.02),
        'cmp_w1_v': nrm(ks[16], (DEPTH, CMP_LEN * HEAD_DIM, CMP_HIDDEN), (CMP_LEN * HEAD_DIM) ** -0.5),
        'cmp_w2_v': nrm(ks[17], (DEPTH, CMP_HIDDEN, HEAD_DIM), CMP_HIDDEN ** -0.5),
        'norm_ffn': 1.0 + nrm(ks[18], (DEPTH, D), 0.02),
        'w_gate': nrm(ks[19], (DEPTH, D, D_FF), D ** -0.5),
        'w_up': nrm(ks[20], (DEPTH, D, D_FF), D ** -0.5),
        'w_down': nrm(ks[21], (DEPTH, D_FF, D), D_FF ** -0.5),
        'norm_final': 1.0 + nrm(ks[22], (D,), 0.02),
    }


def reference(x, norm_mix, w_in, w_out, ret_norm, lru_conv_w, lru_conv_b, lru_wa, lru_ba, lru_wx, lru_bx,
              lru_lambda, cmp_pos_k, cmp_w1_k, cmp_w2_k, cmp_pos_v, cmp_w1_v, cmp_w2_v,
              norm_ffn, w_gate, w_up, w_down, norm_final):
    for l in range(DEPTH):
        x = hybrid_layer(x, norm_mix[l], w_in[l], w_out[l], ret_norm[l], lru_conv_w[l], lru_conv_b[l],
                         lru_wa[l], lru_ba[l], lru_wx[l], lru_bx[l], lru_lambda[l],
                         cmp_pos_k[l], cmp_w1_k[l], cmp_w2_k[l], cmp_pos_v[l], cmp_w1_v[l], cmp_w2_v[l],
                         norm_ffn[l], w_gate[l], w_up[l], w_down[l])
    return rmsnorm(x, norm_final)
```

```python
import functools
import math

import jax
import jax.numpy as jnp
from jax import lax
from jax.experimental import pallas as pl
from jax.experimental.pallas import tpu as pltpu

F32 = jnp.float32
MXU_DTYPE = jnp.bfloat16
EPS = 1e-6
HEAD_DIM = 128
SCALE = HEAD_DIM ** -0.5
GROUP_WIDTH = 1024
GROUP_HEADS = 8
ROPE_BASE = 10000.0
RET_CHUNK = 128
MOBA_BLOCK = 256
MOBA_TOPK = 3
LRU_C = 8.0
CONV_WIDTH = 4
NSA_KV_HEADS = 2
NSA_GROUP = GROUP_HEADS // NSA_KV_HEADS
NSA_BRANCHES = 3
CMP_LEN = 32
CMP_STRIDE = 16
CMP_HIDDEN = 256
SEL_BLOCK = 64
SEL_TOPN = 16
WIN = 512
NEG = -1e30

COL_RQ, COL_RK, COL_RV, COL_RG = 0, 8, 16, 24
COL_MQ, COL_MK, COL_MV = 32, 40, 48
COL_LX, COL_LG = 56, 64
COL_NQ = 72
COL_NKC, COL_NVC, COL_NKS, COL_NVS, COL_NKW, COL_NVW = 80, 82, 84, 86, 88, 90
MAIN_WIDTH = 92 * HEAD_DIM
GATE_PAD = NSA_KV_HEADS * HEAD_DIM

VMEM_LIMIT = 56 * 1024 * 1024


def _mx(x):
    return x.astype(MXU_DTYPE)


def _dot(a, b):
    return jnp.dot(a, b, preferred_element_type=F32)


def _dot_nt(a, b):
    return lax.dot_general(a, b, (((1,), (1,)), ((), ())), preferred_element_type=F32)


def _params(sem):
    return pltpu.CompilerParams(dimension_semantics=sem, vmem_limit_bytes=VMEM_LIMIT)


def _gelu(x):
    return 0.5 * x * (1.0 + jnp.tanh(math.sqrt(2.0 / math.pi) * (x + 0.044715 * (x * x * x))))


def _sigmoid(x):
    return 1.0 / (1.0 + jnp.exp(-x))


def _rmsnorm_kernel(x_ref, w_ref, o_ref):
    x = x_ref[...]
    ms = jnp.mean(x * x, axis=-1, keepdims=True)
    o_ref[...] = (x * lax.rsqrt(ms + EPS) * w_ref[...]).astype(o_ref.dtype)


def _rmsnorm(x, w, out_dtype, tm=256):
    m, d = x.shape
    tm = min(tm, m)
    return pl.pallas_call(
        _rmsnorm_kernel,
        grid=(m // tm,),
        in_specs=[pl.BlockSpec((tm, d), lambda i: (i, 0)), pl.BlockSpec((1, d), lambda i: (0, 0))],
        out_specs=pl.BlockSpec((tm, d), lambda i: (i, 0)),
        out_shape=jax.ShapeDtypeStruct((m, d), out_dtype),
        compiler_params=_params(("parallel",)),
        name="rmsnorm",
    )(x, w.reshape(1, d))


def _inproj_kernel(h_ref, w_ref, wg_ref, o_ref, og_ref):
    h = h_ref[...]
    o_ref[...] = _dot(h, w_ref[...])

    @pl.when(pl.program_id(1) == 0)
    def _():
        og_ref[...] = _dot(h, wg_ref[...])


def _inproj(h, w_main, w_gate, tm=1024, tn=512):
    m, k = h.shape
    n = w_main.shape[1]
    tm, tn = min(tm, m), min(tn, n)
    return pl.pallas_call(
        _inproj_kernel,
        grid=(m // tm, n // tn),
        in_specs=[pl.BlockSpec((tm, k), lambda i, j: (i, 0)),
                  pl.BlockSpec((k, tn), lambda i, j: (0, j)),
                  pl.BlockSpec((k, GATE_PAD), lambda i, j: (0, 0))],
        out_specs=[pl.BlockSpec((tm, tn), lambda i, j: (i, j)),
                   pl.BlockSpec((tm, GATE_PAD), lambda i, j: (i, 0))],
        out_shape=[jax.ShapeDtypeStruct((m, n), F32), jax.ShapeDtypeStruct((m, GATE_PAD), F32)],
        compiler_params=_params(("parallel", "arbitrary")),
        name="inproj",
    )(h, w_main, w_gate)


def _outproj_kernel(y0_ref, y1_ref, y2_ref, y3_ref, w_ref, x_ref, o_ref):
    gw = GROUP_WIDTH
    acc = x_ref[...]
    for g, y_ref in enumerate((y0_ref, y1_ref, y2_ref, y3_ref)):
        acc = acc + _dot(y_ref[...], w_ref[g * gw:(g + 1) * gw, :])
    o_ref[...] = acc


def _outproj(ys, w, x, tm=1024, tn=512):
    m, n = x.shape
    tm, tn = min(tm, m), min(tn, n)
    yspec = pl.BlockSpec((tm, GROUP_WIDTH), lambda i, j: (i, 0))
    return pl.pallas_call(
        _outproj_kernel,
        grid=(m // tm, n // tn),
        in_specs=[yspec, yspec, yspec, yspec,
                  pl.BlockSpec((4 * GROUP_WIDTH, tn), lambda i, j: (0, j)),
                  pl.BlockSpec((tm, tn), lambda i, j: (i, j))],
        out_specs=pl.BlockSpec((tm, tn), lambda i, j: (i, j)),
        out_shape=jax.ShapeDtypeStruct((m, n), F32),
        compiler_params=_params(("parallel", "arbitrary")),
        name="outproj",
    )(*ys, w, x)


def _gateup_kernel(h_ref, wg_ref, wu_ref, o_ref):
    h = h_ref[...]
    g = _dot(h, wg_ref[...])
    u = _dot(h, wu_ref[...])
    o_ref[...] = (g * _sigmoid(g) * u).astype(o_ref.dtype)


def _gateup(h, wg, wu, tm=1024, tn=256):
    m, k = h.shape
    n = wg.shape[1]
    tm, tn = min(tm, m), min(tn, n)
    wspec = pl.BlockSpec((k, tn), lambda i, j: (0, j))
    return pl.pallas_call(
        _gateup_kernel,
        grid=(m // tm, n // tn),
        in_specs=[pl.BlockSpec((tm, k), lambda i, j: (i, 0)), wspec, wspec],
        out_specs=pl.BlockSpec((tm, tn), lambda i, j: (i, j)),
        out_shape=jax.ShapeDtypeStruct((m, n), MXU_DTYPE),
        compiler_params=_params(("parallel", "arbitrary")),
        name="ffn_gateup",
    )(h, wg, wu)


def _down_kernel(u_ref, w_ref, x_ref, o_ref, acc_ref):
    kk = pl.program_id(2)

    @pl.when(kk == 0)
    def _():
        acc_ref[...] = x_ref[...]

    acc_ref[...] += _dot(u_ref[...], w_ref[...])

    @pl.when(kk == pl.num_programs(2) - 1)
    def _():
        o_ref[...] = acc_ref[...]


def _down(u, w, x, tm=1024, tn=512, nk=2):
    m, k = u.shape
    n = w.shape[1]
    tm, tn = min(tm, m), min(tn, n)
    tk = k // nk
    return pl.pallas_call(
        _down_kernel,
        grid=(m // tm, n // tn, nk),
        in_specs=[pl.BlockSpec((tm, tk), lambda i, j, kk: (i, kk)),
                  pl.BlockSpec((tk, tn), lambda i, j, kk: (kk, j)),
                  pl.BlockSpec((tm, tn), lambda i, j, kk: (i, j))],
        out_specs=pl.BlockSpec((tm, tn), lambda i, j, kk: (i, j)),
        out_shape=jax.ShapeDtypeStruct((m, n), F32),
        scratch_shapes=[pltpu.VMEM((tm, tn), F32)],
        compiler_params=_params(("parallel", "arbitrary", "arbitrary")),
        name="ffn_down",
    )(u, w, x)


def _retention_kernel(q_ref, k_ref, v_ref, g_ref, cos_ref, sin_ref, dintra_ref, dte_ref, dfs_ref,
                      dch_ref, gain_ref, o_ref, state_ref):
    @pl.when(pl.program_id(1) == 0)
    def _():
        state_ref[...] = jnp.zeros_like(state_ref)

    cos = cos_ref[...]
    sin = sin_ref[...]
    half = HEAD_DIM // 2
    for h in range(GROUP_HEADS):
        sl = slice(h * HEAD_DIM, (h + 1) * HEAD_DIM)
        q = q_ref[:, sl]
        k = k_ref[:, sl]
        vb = _mx(v_ref[:, sl])
        qr = q * cos + pltpu.roll(q, half, 1) * sin
        kr = (k * cos + pltpu.roll(k, half, 1) * sin) * SCALE
        qb = _mx(qr)
        s = _dot_nt(qb, _mx(kr)) * dintra_ref[h]
        st = state_ref[h]
        y = _dot(_mx(s), vb) + _dot(qb, _mx(st)) * dfs_ref[h]
        kd = kr * dte_ref[h]
        state_ref[h] = dch_ref[h] * st + _dot(_mx(kd.T), vb)
        y = y - jnp.mean(y, axis=-1, keepdims=True)
        y = y * lax.rsqrt(jnp.mean(y * y, axis=-1, keepdims=True) + EPS)
        g = g_ref[:, sl]
        o_ref[:, sl] = (g * _sigmoid(g) * (y * gain_ref[:, sl])).astype(o_ref.dtype)


def _retention(proj, gain, batch, seq):
    c = RET_CHUNK
    n = seq // c
    h = GROUP_HEADS
    half = HEAD_DIM // 2
    pos = jnp.arange(seq).astype(F32)
    inv = ROPE_BASE ** (-jnp.arange(half, dtype=F32) / half)
    ang = pos[:, None] * inv[None, :]
    cos = jnp.cos(ang)
    sin = jnp.sin(ang)
    cos_t = jnp.concatenate([cos, cos], axis=-1)
    sin_t = jnp.concatenate([-sin, sin], axis=-1)
    log_gamma = jnp.log1p(-jnp.exp2(-5.0 - jnp.arange(h, dtype=F32)))
    pc = jnp.arange(c, dtype=F32)
    diff = pc[:, None] - pc[None, :]
    dintra = jnp.where(diff >= 0, jnp.exp(log_gamma[:, None, None] * jnp.maximum(diff, 0.0)), 0.0)
    dte = jnp.exp(log_gamma[:, None] * (c - 1.0 - pc)[None, :])
    dfs = jnp.exp(log_gamma[:, None] * (pc + 1.0)[None, :])
    dte = jnp.broadcast_to(dte[:, :, None], (h, c, HEAD_DIM))
    dfs = jnp.broadcast_to(dfs[:, :, None], (h, c, HEAD_DIM))
    dch = jnp.broadcast_to(jnp.exp(log_gamma * c)[:, None, None], (h, 1, HEAD_DIM))

    def col(cb):
        return pl.BlockSpec((c, GROUP_WIDTH), lambda b, i: (b * n + i, cb))

    full3 = lambda shape: pl.BlockSpec(shape, lambda b, i: (0, 0, 0))
    tab = pl.BlockSpec((c, HEAD_DIM), lambda b, i: (i, 0))
    return pl.pallas_call(
        _retention_kernel,
        grid=(batch, n),
        in_specs=[col(COL_RQ // 8), col(COL_RK // 8), col(COL_RV // 8), col(COL_RG // 8), tab, tab,
                  full3((h, c, c)), full3((h, c, HEAD_DIM)), full3((h, c, HEAD_DIM)),
                  full3((h, 1, HEAD_DIM)),
                  pl.BlockSpec((1, GROUP_WIDTH), lambda b, i: (0, 0))],
        out_specs=pl.BlockSpec((c, GROUP_WIDTH), lambda b, i: (b * n + i, 0)),
        out_shape=jax.ShapeDtypeStruct((batch * seq, GROUP_WIDTH), MXU_DTYPE),
        scratch_shapes=[pltpu.VMEM((h, HEAD_DIM, HEAD_DIM), F32)],
        compiler_params=_params(("parallel", "arbitrary")),
        name="retention",
    )(proj, proj, proj, proj, cos_t, sin_t, dintra, dte, dfs, dch, gain.reshape(1, GROUP_WIDTH))


def _moba_kernel(q_ref, k_ref, v_ref, o_ref, kmean_ref, *, nb, topk):
    blk = MOBA_BLOCK
    j = pl.program_id(2)

    @pl.when(j == 0)
    def _():
        kmean_ref[...] = jnp.zeros_like(kmean_ref)
        for n in range(nb):
            kmean_ref[n:n + 1, :] = jnp.mean(k_ref[n * blk:(n + 1) * blk, :], axis=0, keepdims=True)

    qb = _mx(q_ref[...])
    lane = lax.broadcasted_iota(jnp.int32, (blk, HEAD_DIM), 1)
    gate = _dot_nt(qb, _mx(kmean_ref[...]))
    past = lane < j
    gcur = jnp.where(past, gate, -jnp.inf)
    sel = jnp.zeros((blk, HEAD_DIM), F32)
    for _ in range(topk):
        mx = jnp.max(gcur, axis=1, keepdims=True)
        idx = jnp.min(jnp.where(gcur == mx, lane, HEAD_DIM), axis=1, keepdims=True)
        pick = lane == idx
        sel = jnp.where(pick, 1.0, sel)
        gcur = jnp.where(pick, -jnp.inf, gcur)
    sel = jnp.where(past, sel, 0.0)

    def attend(carry, kb, vb, valid):
        m, l, acc = carry
        s = jnp.where(valid, _dot_nt(qb, kb) * SCALE, NEG)
        m_new = jnp.maximum(m, jnp.max(s, axis=1, keepdims=True))
        p = jnp.where(valid, jnp.exp(s - m_new), 0.0)
        alpha = jnp.exp(m - m_new)
        l = alpha * l + jnp.sum(p, axis=1, keepdims=True)
        acc = alpha * acc + _dot(_mx(p), vb)
        return m_new, l, acc

    def body(n, carry):
        off = pl.multiple_of(n * blk, blk)
        kb = _mx(k_ref[pl.ds(off, blk), :])
        vb = _mx(v_ref[pl.ds(off, blk), :])
        seln = jnp.sum(jnp.where(lane == n, sel, 0.0), axis=1, keepdims=True) > 0.5
        return attend(carry, kb, vb, seln)

    init = (jnp.full((blk, 1), NEG, F32), jnp.zeros((blk, 1), F32), jnp.zeros((blk, HEAD_DIM), F32))
    carry = lax.fori_loop(0, j, body, init)
    off = pl.multiple_of(j * blk, blk)
    row = lax.broadcasted_iota(jnp.int32, (blk, blk), 0)
    colk = lax.broadcasted_iota(jnp.int32, (blk, blk), 1)
    _, l, acc = attend(carry, _mx(k_ref[pl.ds(off, blk), :]), _mx(v_ref[pl.ds(off, blk), :]), colk <= row)
    o_ref[...] = (acc / l).astype(o_ref.dtype)


def _moba(proj, batch, seq):
    blk = MOBA_BLOCK
    nb = seq // blk
    kv = lambda cb: pl.BlockSpec((seq, HEAD_DIM), lambda b, h, j: (b, cb + h))
    return pl.pallas_call(
        functools.partial(_moba_kernel, nb=nb, topk=min(MOBA_TOPK, nb)),
        grid=(batch, GROUP_HEADS, nb),
        in_specs=[pl.BlockSpec((blk, HEAD_DIM), lambda b, h, j: (b * nb + j, COL_MQ + h)),
                  kv(COL_MK), kv(COL_MV)],
        out_specs=pl.BlockSpec((blk, HEAD_DIM), lambda b, h, j: (b * nb + j, h)),
        out_shape=jax.ShapeDtypeStruct((batch * seq, GROUP_WIDTH), MXU_DTYPE),
        scratch_shapes=[pltpu.VMEM((HEAD_DIM, HEAD_DIM), F32)],
        compiler_params=_params(("parallel", "parallel", "arbitrary")),
        name="moba",
    )(proj, proj, proj)


def _lru_kernel(x_ref, xp_ref, g_ref, cw_ref, cb_ref, wa_ref, ba_ref, wx_ref, bx_ref, lam_ref, o_ref,
                xs_ref, a_ref, u_ref, hc_ref, *, rows):
    t = pl.program_id(1)
    pad = 8

    @pl.when(t == 0)
    def _():
        hc_ref[...] = jnp.zeros_like(hc_ref)

    xs_ref[0:pad, :] = jnp.where(t == 0, 0.0, xp_ref[...])
    xs_ref[pad:pad + rows, :] = x_ref[...]
    xc = jnp.broadcast_to(cb_ref[...], (rows, GROUP_WIDTH))
    for tap in range(CONV_WIDTH):
        start = pad - (CONV_WIDTH - 1) + tap
        xc = xc + xs_ref[start:start + rows, :] * cw_ref[tap:tap + 1, :]
    nlam = -lam_ref[...]
    sp = jnp.maximum(nlam, 0.0) + jnp.log1p(jnp.exp(-jnp.abs(nlam)))
    for gb in range(GROUP_HEADS):
        sl = slice(gb * HEAD_DIM, (gb + 1) * HEAD_DIM)
        xg = xc[:, sl]
        xb = _mx(xg)
        r = _sigmoid(_dot(xb, wa_ref[gb]) + ba_ref[:, sl])
        i = _sigmoid(_dot(xb, wx_ref[gb]) + bx_ref[:, sl])
        log_a = -LRU_C * r * sp[:, sl]
        a_ref[:, sl] = jnp.exp(log_a)
        u_ref[:, sl] = jnp.sqrt(jnp.maximum(-jnp.expm1(2.0 * log_a), 0.0)) * (i * xg)

    srow = lax.broadcasted_iota(jnp.int32, (8, GROUP_WIDTH), 0)

    def body(s, hprev):
        off = pl.multiple_of(s * 8, 8)
        a = a_ref[pl.ds(off, 8), :]
        u = u_ref[pl.ds(off, 8), :]
        for sh in (1, 2, 4):
            ok = srow >= sh
            u = jnp.where(ok, a * pltpu.roll(u, sh, 0) + u, u)
            a = jnp.where(ok, a * pltpu.roll(a, sh, 0), a)
        hh = a * hprev + u
        u_ref[pl.ds(off, 8), :] = hh
        return hh[7:8, :]

    hc_ref[...] = lax.fori_loop(0, rows // 8, body, hc_ref[...])
    o_ref[...] = (u_ref[...] * _gelu(g_ref[...])).astype(o_ref.dtype)


def _lru(proj, conv_w, conv_b, wa, ba, wx, bx, lam, batch, seq, rows=256):
    rows = min(rows, seq)
    nt = seq // rows
    gw = GROUP_WIDTH
    vec = pl.BlockSpec((1, gw), lambda b, t: (0, 0))
    wsp = pl.BlockSpec((GROUP_HEADS, HEAD_DIM, HEAD_DIM), lambda b, t: (0, 0, 0))
    return pl.pallas_call(
        functools.partial(_lru_kernel, rows=rows),
        grid=(batch, nt),
        in_specs=[pl.BlockSpec((rows, gw), lambda b, t: (b * nt + t, COL_LX // 8)),
                  pl.BlockSpec((8, gw), lambda b, t: (jnp.maximum((b * nt + t) * (rows // 8) - 1, 0), COL_LX // 8)),
                  pl.BlockSpec((rows, gw), lambda b, t: (b * nt + t, COL_LG // 8)),
                  pl.BlockSpec((CONV_WIDTH, gw), lambda b, t: (0, 0)), vec, wsp, vec, wsp, vec, vec],
        out_specs=pl.BlockSpec((rows, gw), lambda b, t: (b * nt + t, 0)),
        out_shape=jax.ShapeDtypeStruct((batch * seq, gw), MXU_DTYPE),
        scratch_shapes=[pltpu.VMEM((rows + 8, gw), F32), pltpu.VMEM((rows, gw), F32),
                        pltpu.VMEM((rows, gw), F32), pltpu.VMEM((1, gw), F32)],
        compiler_params=_params(("parallel", "arbitrary")),
        name="rg_lru",
    )(proj, proj, proj, conv_w, conv_b.reshape(1, gw), wa, ba.reshape(1, gw), wx, bx.reshape(1, gw),
      lam.reshape(1, gw))


def _compress(x_ref, pos_ref, w1_ref, w2_ref, ncp):
    p1 = jnp.zeros((ncp, CMP_HIDDEN), F32)
    p2 = jnp.zeros((ncp, CMP_HIDDEN), F32)
    for l in range(CMP_STRIDE):
        xs = x_ref[pl.ds(l, ncp, stride=CMP_STRIDE), :]
        lo = l * HEAD_DIM
        hi = (CMP_STRIDE + l) * HEAD_DIM
        p1 = p1 + _dot(_mx(xs + pos_ref[l:l + 1, :]), w1_ref[lo:lo + HEAD_DIM, :])
        p2 = p2 + _dot(_mx(xs + pos_ref[CMP_STRIDE + l:CMP_STRIDE + l + 1, :]), w1_ref[hi:hi + HEAD_DIM, :])
    pre = p1 + pltpu.roll(p2, ncp - 1, 0)
    return _dot(_mx(_gelu(pre)), w2_ref[...])


def _nsa_cmp_kernel(q_ref, xk_ref, xv_ref, pk_ref, w1k_ref, w2k_ref, pv_ref, w1v_ref, w2v_ref, ov_ref,
                    o_ref, sel_ref, kc_ref, vc_ref, *, tq, ncp, nsel, topn):
    i = pl.program_id(2)

    @pl.when(i == 0)
    def _():
        kc_ref[...] = _compress(xk_ref, pk_ref, w1k_ref, w2k_ref, ncp).astype(kc_ref.dtype)
        vc_ref[...] = _compress(xv_ref, pv_ref, w1v_ref, w2v_ref, ncp).astype(vc_ref.dtype)

    kc = kc_ref[...]
    vc = vc_ref[...]
    ovt = ov_ref[...]
    t_row = i * tq + lax.broadcasted_iota(jnp.int32, (tq, ncp), 0)
    blk_end = CMP_STRIDE * lax.broadcasted_iota(jnp.int32, (tq, ncp), 1) + (CMP_LEN - 1)
    mask = blk_end <= t_row
    imp_t = jnp.zeros((nsel, tq), F32)
    for g in range(NSA_GROUP):
        sl = slice(g * HEAD_DIM, (g + 1) * HEAD_DIM)
        s = jnp.where(mask, _dot_nt(_mx(q_ref[:, sl]), kc) * SCALE, NEG)
        m = jnp.max(s, axis=1, keepdims=True)
        e = jnp.where(mask, jnp.exp(s - m), 0.0)
        p = _mx(e / jnp.maximum(jnp.sum(e, axis=1, keepdims=True), 1e-30))
        o_ref[:, sl] = _dot(p, vc)
        imp_t = imp_t + _dot_nt(ovt, p)

    jrow = lax.broadcasted_iota(jnp.int32, (nsel, tq), 0)
    qsb = (i * tq + lax.broadcasted_iota(jnp.int32, (nsel, tq), 1)) >> int(math.log2(SEL_BLOCK))
    forced = (jrow == 0) | (jrow == qsb) | (jrow == qsb - 1)
    val = jnp.where(forced, jnp.inf, jnp.where(jrow <= qsb, imp_t, -jnp.inf))
    rank = jnp.zeros((nsel, tq), F32)
    for mblk in range(nsel):
        vm = val[mblk:mblk + 1, :]
        beats = jnp.where(jrow > mblk, vm >= val, vm > val)
        rank = rank + jnp.where(beats, 1.0, 0.0)
    sel_t = jnp.where(rank < topn, 1.0, 0.0)
    if nsel < HEAD_DIM:
        sel_t = jnp.concatenate([sel_t, jnp.zeros((HEAD_DIM - nsel, tq), F32)], axis=0)
    sel_ref[...] = sel_t.T


def _nsa_cmp(proj, pos_k, w1_k, w2_k, pos_v, w1_v, w2_v, batch, seq, tq=256):
    tq = min(tq, seq)
    nq = seq // tq
    ncp = seq // CMP_STRIDE
    nsel = seq // SEL_BLOCK
    nc = (seq - CMP_LEN) // CMP_STRIDE + 1
    starts = jnp.arange(ncp) * CMP_STRIDE
    sel_start = jnp.arange(nsel) * SEL_BLOCK
    overlap_t = ((starts[None, :] < sel_start[:, None] + SEL_BLOCK)
                 & (starts[None, :] + CMP_LEN > sel_start[:, None])
                 & (jnp.arange(ncp)[None, :] < nc)).astype(MXU_DTYPE)
    gq = NSA_GROUP * HEAD_DIM
    full = lambda shape: pl.BlockSpec(shape, lambda b, k, i: (0, 0))
    kvspec = lambda cb: pl.BlockSpec((seq, HEAD_DIM), lambda b, k, i: (b, cb + k))
    return pl.pallas_call(
        functools.partial(_nsa_cmp_kernel, tq=tq, ncp=ncp, nsel=nsel, topn=min(SEL_TOPN, nsel)),
        grid=(batch, NSA_KV_HEADS, nq),
        in_specs=[pl.BlockSpec((tq, gq), lambda b, k, i: (b * nq + i, COL_NQ // NSA_GROUP + k)),
                  kvspec(COL_NKC), kvspec(COL_NVC),
                  full((CMP_LEN, HEAD_DIM)), full((CMP_LEN * HEAD_DIM, CMP_HIDDEN)), full((CMP_HIDDEN, HEAD_DIM)),
                  full((CMP_LEN, HEAD_DIM)), full((CMP_LEN * HEAD_DIM, CMP_HIDDEN)), full((CMP_HIDDEN, HEAD_DIM)),
                  full((nsel, ncp))],
        out_specs=[pl.BlockSpec((tq, gq), lambda b, k, i: (b * nq + i, k)),
                   pl.BlockSpec((tq, HEAD_DIM), lambda b, k, i: (b * nq + i, k))],
        out_shape=[jax.ShapeDtypeStruct((batch * seq, GROUP_WIDTH), F32),
                   jax.ShapeDtypeStruct((batch * seq, NSA_KV_HEADS * HEAD_DIM), F32)],
        scratch_shapes=[pltpu.VMEM((ncp, HEAD_DIM), MXU_DTYPE), pltpu.VMEM((ncp, HEAD_DIM), MXU_DTYPE)],
        compiler_params=_params(("parallel", "parallel", "arbitrary")),
        name="nsa_compress",
    )(proj, proj, proj, pos_k, w1_k, w2_k, pos_v, w1_v, w2_v, overlap_t)


def _nsa_attn_kernel(q_ref, ks_ref, vs_ref, kw_ref, vw_ref, sel_ref, oc_ref, gl_ref, ex_ref, o_ref,
                     bias_ref, *, tq, tk, seq):
    i = pl.program_id(2)
    ntiles = (i * tq + tq - 1) // tk + 1
    t_q = i * tq + lax.broadcasted_iota(jnp.int32, (tq, tk), 0)
    kcol = lax.broadcasted_iota(jnp.int32, (tq, tk), 1)
    selb = _mx(sel_ref[...])

    def fill(n, c):
        hit = _dot(selb, ex_ref[n]) > 0.5
        bias_ref[n] = jnp.where(hit & (n * tk + kcol <= t_q), 0.0, NEG)
        return c

    lax.fori_loop(0, ntiles, fill, 0)

    wlen = WIN + tq
    wstart = pl.multiple_of(jnp.maximum(i * tq - WIN, 0), tq)
    kwb = _mx(kw_ref[pl.ds(wstart, wlen), :])
    vwb = _mx(vw_ref[pl.ds(wstart, wlen), :])
    t_w = i * tq + lax.broadcasted_iota(jnp.int32, (tq, wlen), 0)
    dist = t_w - (wstart + lax.broadcasted_iota(jnp.int32, (tq, wlen), 1))
    wmask = (dist >= 0) & (dist < WIN)
    gates = _sigmoid(gl_ref[...])

    for g in range(NSA_GROUP):
        sl = slice(g * HEAD_DIM, (g + 1) * HEAD_DIM)
        qb = _mx(q_ref[:, sl])

        def body(n, carry):
            m, l, acc = carry
            off = pl.multiple_of(n * tk, tk)
            s = _dot_nt(qb, _mx(ks_ref[pl.ds(off, tk), :])) * SCALE + bias_ref[n]
            m_new = jnp.maximum(m, jnp.max(s, axis=1, keepdims=True))
            p = jnp.exp(s - m_new)
            alpha = jnp.exp(m - m_new)
            l = alpha * l + jnp.sum(p, axis=1, keepdims=True)
            acc = alpha * acc + _dot(_mx(p), _mx(vs_ref[pl.ds(off, tk), :]))
            return m_new, l, acc

        init = (jnp.full((tq, 1), NEG, F32), jnp.zeros((tq, 1), F32), jnp.zeros((tq, HEAD_DIM), F32))
        _, l, acc = lax.fori_loop(0, ntiles, body, init)
        o_sel = acc / l

        s = jnp.where(wmask, _dot_nt(qb, kwb) * SCALE, NEG)
        m = jnp.max(s, axis=1, keepdims=True)
        e = jnp.where(wmask, jnp.exp(s - m), 0.0)
        p = e / jnp.maximum(jnp.sum(e, axis=1, keepdims=True), 1e-30)
        o_win = _dot(_mx(p), vwb)

        c0 = g * NSA_BRANCHES
        o = (gates[:, c0:c0 + 1] * oc_ref[:, sl] + gates[:, c0 + 1:c0 + 2] * o_sel
             + gates[:, c0 + 2:c0 + 3] * o_win)
        o_ref[:, sl] = o.astype(o_ref.dtype)


def _nsa_attn(proj, sel, o_cmp, gate_logits, batch, seq, tq=128, tk=256):
    tk = min(tk, seq)
    nq = seq // tq
    nkt = seq // tk
    gq = NSA_GROUP * HEAD_DIM
    key_blk = jnp.arange(seq) // SEL_BLOCK
    expand = (jnp.arange(HEAD_DIM)[:, None] == key_blk[None, :]).astype(MXU_DTYPE)
    expand = expand.reshape(HEAD_DIM, nkt, tk).transpose(1, 0, 2)
    kvspec = lambda cb: pl.BlockSpec((seq, HEAD_DIM), lambda b, k, i: (b, cb + k))
    return pl.pallas_call(
        functools.partial(_nsa_attn_kernel, tq=tq, tk=tk, seq=seq),
        grid=(batch, NSA_KV_HEADS, nq),
        in_specs=[pl.BlockSpec((tq, gq), lambda b, k, i: (b * nq + i, COL_NQ // NSA_GROUP + k)),
                  kvspec(COL_NKS), kvspec(COL_NVS), kvspec(COL_NKW), kvspec(COL_NVW),
                  pl.BlockSpec((tq, HEAD_DIM), lambda b, k, i: (b * nq + i, k)),
                  pl.BlockSpec((tq, gq), lambda b, k, i: (b * nq + i, k)),
                  pl.BlockSpec((tq, HEAD_DIM), lambda b, k, i: (b * nq + i, k)),
                  pl.BlockSpec((nkt, HEAD_DIM, tk), lambda b, k, i: (0, 0, 0))],
        out_specs=pl.BlockSpec((tq, gq), lambda b, k, i: (b * nq + i, k)),
        out_shape=jax.ShapeDtypeStruct((batch * seq, GROUP_WIDTH), MXU_DTYPE),
        scratch_shapes=[pltpu.VMEM((nkt, tq, tk), F32)],
        compiler_params=_params(("parallel", "parallel", "arbitrary")),
        name="nsa_attend",
    )(proj, proj, proj, proj, proj, sel, o_cmp, gate_logits, expand)


MOBA_QCHUNK = 32
NSA_QCHUNK = 64
WIN_QBLOCK = 128


def _x_softmax(scores, mask):
    s = jnp.where(mask, scores.astype(F32), -jnp.inf)
    m = jnp.max(s, axis=-1, keepdims=True)
    m = jnp.where(jnp.isfinite(m), m, 0.0)
    e = jnp.where(mask, jnp.exp(s - m), 0.0)
    return e / jnp.maximum(jnp.sum(e, axis=-1, keepdims=True), 1e-30)


def _x_to_chunks(t, axis, size):
    n = t.shape[axis] // size
    t = t.reshape(t.shape[:axis] + (n, size) + t.shape[axis + 1:])
    return jnp.moveaxis(t, axis, 0)


def _x_from_chunks(t, axis):
    t = jnp.moveaxis(t, 0, axis)
    return t.reshape(t.shape[:axis] + (t.shape[axis] * t.shape[axis + 1],) + t.shape[axis + 2:])


def _x_rope(x, pos):
    half = x.shape[-1] // 2
    inv = ROPE_BASE ** (-jnp.arange(half, dtype=F32) / half)
    ang = pos.astype(F32)[:, None] * inv[None, :]
    cos = jnp.cos(ang)[None, :, None, :]
    sin = jnp.sin(ang)[None, :, None, :]
    x1, x2 = x[..., :half], x[..., half:]
    return jnp.concatenate([x1 * cos - x2 * sin, x1 * sin + x2 * cos], axis=-1)


def _x_retention(q, k, v, g, gain):
    B_, S_, H, d = q.shape
    pos = jnp.arange(S_)
    q = _x_rope(q.astype(F32), pos)
    k = _x_rope(k.astype(F32), pos) * (d ** -0.5)
    v = v.astype(F32)
    C = RET_CHUNK
    N = S_ // C
    log_gamma = jnp.log1p(-jnp.exp2(-5.0 - jnp.arange(H, dtype=F32)))
    qc = q.reshape(B_, N, C, H, d)
    kc = k.reshape(B_, N, C, H, d)
    vc = v.reshape(B_, N, C, H, d)
    pc = jnp.arange(C, dtype=F32)
    diff = pc[:, None] - pc[None, :]
    decay_intra = jnp.where(diff >= 0, jnp.exp(log_gamma[:, None, None] * jnp.maximum(diff, 0.0)), 0.0)
    scores = jnp.einsum('bnihd,bnjhd->bnhij', qc, kc) * decay_intra
    y_intra = jnp.einsum('bnhij,bnjhd->bnihd', scores, vc)
    decay_to_end = jnp.exp(log_gamma[:, None] * (C - 1.0 - pc)[None, :])
    kv_chunk = jnp.einsum('bnjhd,hj,bnjhe->bnhde', kc, decay_to_end, vc)
    decay_chunk = jnp.exp(log_gamma * C)[None, :, None, None]

    def step(state, kv_n):
        return decay_chunk * state + kv_n, state

    _, state_prev = lax.scan(step, jnp.zeros((B_, H, d, d), F32), jnp.moveaxis(kv_chunk, 1, 0))
    state_prev = jnp.moveaxis(state_prev, 0, 1)
    decay_from_start = jnp.exp(log_gamma[:, None] * (pc + 1.0)[None, :])
    y_cross = jnp.einsum('bnihd,bnhde,hi->bnihe', qc, state_prev, decay_from_start)
    y = (y_intra + y_cross).reshape(B_, S_, H, d)
    y = y - jnp.mean(y, axis=-1, keepdims=True)
    y = y * lax.rsqrt(jnp.mean(y * y, axis=-1, keepdims=True) + EPS)
    y = y.reshape(B_, S_, H * d) * gain.astype(F32)
    return (jax.nn.silu(g.astype(F32)) * y).astype(g.dtype)


def _x_moba(q, k, v):
    B_, S_, H, d = q.shape
    scale = d ** -0.5
    BLK = MOBA_BLOCK
    QC = MOBA_QCHUNK
    NB = -(-S_ // BLK)
    pad = NB * BLK - S_
    qt = q.transpose(0, 2, 1, 3)
    kp = jnp.pad(k.transpose(0, 2, 1, 3), ((0, 0), (0, 0), (0, pad), (0, 0)))
    vp = jnp.pad(v.transpose(0, 2, 1, 3), ((0, 0), (0, 0), (0, pad), (0, 0)))
    kb = kp.reshape(B_, H, NB, BLK, d)
    vb = vp.reshape(B_, H, NB, BLK, d)
    k_mean = jnp.mean(kb.astype(F32), axis=3)
    qblk = jnp.arange(S_) // BLK
    gate = jnp.einsum('bhsd,bhnd->bhsn', qt.astype(F32), k_mean)
    past = jnp.arange(NB)[None, :] < qblk[:, None]
    gate = jnp.where(past, gate, -jnp.inf)
    topk = min(MOBA_TOPK, NB)
    _, sel = lax.top_k(gate, topk)
    sel_valid = sel < qblk[None, None, :, None]
    bi = jnp.arange(B_)[:, None, None, None]
    hi = jnp.arange(H)[None, :, None, None]
    M = topk * BLK

    def one_chunk(args):
        c, q_c, sel_c, val_c = args
        t_c = c * QC + jnp.arange(QC)
        k_sel = kb[bi, hi, sel_c].reshape(B_, H, QC, M, d)
        v_sel = vb[bi, hi, sel_c].reshape(B_, H, QC, M, d)
        own = (c * QC) // BLK * BLK
        k_own = lax.dynamic_slice_in_dim(kp, own, BLK, axis=2)
        v_own = lax.dynamic_slice_in_dim(vp, own, BLK, axis=2)
        s_sel = jnp.einsum('bhqd,bhqmd->bhqm', q_c, k_sel)
        s_own = jnp.einsum('bhqd,bhld->bhql', q_c, k_own)
        m_sel = jnp.broadcast_to(val_c[..., None], (B_, H, QC, topk, BLK)).reshape(B_, H, QC, M)
        m_own = jnp.broadcast_to(((own + jnp.arange(BLK))[None, :] <= t_c[:, None])[None, None], (B_, H, QC, BLK))
        p = _x_softmax(jnp.concatenate([s_sel, s_own], axis=-1) * scale,
                       jnp.concatenate([m_sel, m_own], axis=-1)).astype(v.dtype)
        return (jnp.einsum('bhqm,bhqmd->bhqd', p[..., :M], v_sel)
                + jnp.einsum('bhql,bhld->bhqd', p[..., M:], v_own))

    n_ch = S_ // QC
    out = lax.map(one_chunk, (jnp.arange(n_ch), _x_to_chunks(qt, 2, QC), _x_to_chunks(sel, 2, QC),
                              _x_to_chunks(sel_valid, 2, QC)))
    out = _x_from_chunks(out, 2)
    return out.transpose(0, 2, 1, 3).reshape(B_, S_, H * d)


def _x_lru(xb, gb, conv_w, conv_b, wa, ba, wx, bx, lam):
    B_, S_, R = xb.shape
    xp = jnp.pad(xb, ((0, 0), (CONV_WIDTH - 1, 0), (0, 0)))
    xc = conv_b
    for tap in range(CONV_WIDTH):
        xc = xc + xp[:, tap:tap + S_] * conv_w[tap]
    xg = xc.reshape(B_, S_, GROUP_HEADS, HEAD_DIM)
    r = jax.nn.sigmoid((jnp.einsum('bsgi,gij->bsgj', xg, wa).reshape(B_, S_, R) + ba).astype(F32))
    i = jax.nn.sigmoid((jnp.einsum('bsgi,gij->bsgj', xg, wx).reshape(B_, S_, R) + bx).astype(F32))
    log_a = -LRU_C * r * jax.nn.softplus(-lam.astype(F32))
    a = jnp.exp(log_a)
    u = jnp.sqrt(jnp.maximum(-jnp.expm1(2.0 * log_a), 0.0)) * (i * xc.astype(F32))
    _, h = lax.associative_scan(lambda e1, e2: (e1[0] * e2[0], e2[0] * e1[1] + e2[1]), (a, u), axis=1)
    return (h * jax.nn.gelu(gb.astype(F32))).astype(xb.dtype)


def _x_nsa(q, k_cmp, v_cmp, k_sel, v_sel, k_win, v_win, gate_logits, pos_k, w1_k, w2_k, pos_v, w1_v, w2_v):
    B_, S_, H, d = q.shape
    KVH = k_cmp.shape[2]
    G = H // KVH
    scale = d ** -0.5
    t = jnp.arange(S_)
    qg = q.reshape(B_, S_, KVH, G, d).transpose(0, 2, 3, 1, 4)
    NC = (S_ - CMP_LEN) // CMP_STRIDE + 1
    starts = jnp.arange(NC) * CMP_STRIDE
    cidx = starts[:, None] + jnp.arange(CMP_LEN)[None, :]

    def compress(x, pos, w1, w2):
        blk = x[:, cidx] + pos[None, None, :, None, :]
        flat = blk.transpose(0, 3, 1, 2, 4).reshape(B_, KVH, NC, CMP_LEN * d)
        return jax.nn.gelu(flat @ w1) @ w2

    kc = compress(k_cmp, pos_k, w1_k, w2_k)
    vc = compress(v_cmp, pos_v, w1_v, w2_v)
    s_cmp = jnp.einsum('bkgsd,bknd->bkgsn', qg, kc) * scale
    m_cmp = (starts + CMP_LEN - 1)[None, :] <= t[:, None]
    p_cmp = _x_softmax(s_cmp, m_cmp)
    o_cmp = jnp.einsum('bkgsn,bknd->bkgsd', p_cmp.astype(vc.dtype), vc).astype(F32)
    NSEL = S_ // SEL_BLOCK
    sel_start = jnp.arange(NSEL) * SEL_BLOCK
    overlap = ((starts[:, None] < sel_start[None, :] + SEL_BLOCK)
               & (starts[:, None] + CMP_LEN > sel_start[None, :])).astype(F32)
    imp = jnp.einsum('bkgsn,nj->bksj', p_cmp, overlap)
    qsb = t // SEL_BLOCK
    j = jnp.arange(NSEL)[None, :]
    forced = (j == 0) | (j == qsb[:, None]) | (j == qsb[:, None] - 1)
    allowed = j <= qsb[:, None]
    imp = jnp.where(forced, jnp.inf, jnp.where(allowed, imp, -jnp.inf))
    topn = min(SEL_TOPN, NSEL)
    _, sel = lax.top_k(imp, topn)
    kb = k_sel.transpose(0, 2, 1, 3).reshape(B_, KVH, NSEL, SEL_BLOCK, d)
    vb = v_sel.transpose(0, 2, 1, 3).reshape(B_, KVH, NSEL, SEL_BLOCK, d)
    bi = jnp.arange(B_)[:, None, None, None]
    hi = jnp.arange(KVH)[None, :, None, None]
    QC = NSA_QCHUNK
    M = topn * SEL_BLOCK

    def sel_chunk(args):
        c, q_c, sel_c = args
        t_c = c * QC + jnp.arange(QC)
        kk = kb[bi, hi, sel_c].reshape(B_, KVH, QC, M, d)
        vv = vb[bi, hi, sel_c].reshape(B_, KVH, QC, M, d)
        s = jnp.einsum('bkgqd,bkqmd->bkgqm', q_c, kk) * scale
        kpos = (sel_c[..., None] * SEL_BLOCK + jnp.arange(SEL_BLOCK)).reshape(B_, KVH, QC, M)
        mask = (kpos <= t_c[None, None, :, None])[:, :, None]
        p = _x_softmax(s, mask).astype(vv.dtype)
        return jnp.einsum('bkgqm,bkqmd->bkgqd', p, vv)

    n_ch = S_ // QC
    o_sel = lax.map(sel_chunk, (jnp.arange(n_ch), _x_to_chunks(qg, 3, QC), _x_to_chunks(sel, 2, QC)))
    o_sel = _x_from_chunks(o_sel, 3).astype(F32)
    NQB = S_ // WIN_QBLOCK
    NPB = WIN // WIN_QBLOCK

    def window_blocks(x):
        xp = jnp.pad(x.transpose(0, 2, 1, 3), ((0, 0), (0, 0), (WIN, 0), (0, 0)))
        xp = xp.reshape(B_, KVH, NPB + NQB, WIN_QBLOCK, d)
        return jnp.concatenate([xp[:, :, o:o + NQB] for o in range(NPB + 1)], axis=3)

    kw = window_blocks(k_win)
    vw = window_blocks(v_win)
    qw = qg.reshape(B_, KVH, G, NQB, WIN_QBLOCK, d)
    s_win = jnp.einsum('bkgnqd,bknmd->bkgnqm', qw, kw) * scale
    tq = jnp.arange(S_).reshape(NQB, WIN_QBLOCK)
    kpos = (jnp.arange(NQB) * WIN_QBLOCK - WIN)[:, None] + jnp.arange((NPB + 1) * WIN_QBLOCK)[None, :]
    dist = tq[:, :, None] - kpos[:, None, :]
    m_win = (kpos[:, None, :] >= 0) & (dist >= 0) & (dist < WIN)
    p_win = _x_softmax(s_win, m_win).astype(vw.dtype)
    o_win = jnp.einsum('bkgnqm,bknmd->bkgnqd', p_win, vw).reshape(B_, KVH, G, S_, d).astype(F32)
    gates = jax.nn.sigmoid(gate_logits.astype(F32)).reshape(B_, S_, KVH, G, NSA_BRANCHES).transpose(0, 2, 3, 1, 4)
    o = gates[..., 0:1] * o_cmp + gates[..., 1:2] * o_sel + gates[..., 2:3] * o_win
    return o.transpose(0, 3, 1, 2, 4).reshape(B_, S_, H * d).astype(q.dtype)


def _split_w_in(w_in):
    w_main = w_in[:, :MAIN_WIDTH].astype(MXU_DTYPE)
    per_kv = NSA_GROUP * NSA_BRANCHES
    gate = w_in[:, MAIN_WIDTH:].reshape(w_in.shape[0], NSA_KV_HEADS, per_kv)
    gate = jnp.pad(gate, ((0, 0), (0, 0), (0, HEAD_DIM - per_kv)))
    return w_main, gate.reshape(w_in.shape[0], GATE_PAD).astype(MXU_DTYPE)


def _layer(x, batch, seq, norm_mix, w_in, w_out, ret_norm, lru_conv_w, lru_conv_b, lru_wa, lru_ba, lru_wx,
           lru_bx, lru_lambda, cmp_pos_k, cmp_w1_k, cmp_w2_k, cmp_pos_v, cmp_w1_v, cmp_w2_v,
           norm_ffn, w_gate, w_up, w_down):
    w_main, w_ngate = _split_w_in(w_in)
    h = _rmsnorm(x, norm_mix, MXU_DTYPE)
    proj, gate_logits = _inproj(h, w_main, w_ngate)
    p3 = proj.reshape(batch, seq, MAIN_WIDTH)

    def heads(c0, n):
        return p3[:, :, c0 * HEAD_DIM:(c0 + n) * HEAD_DIM].reshape(batch, seq, n, HEAD_DIM)

    def flat(c0, n):
        return p3[:, :, c0 * HEAD_DIM:(c0 + n) * HEAD_DIM]

    per_kv = NSA_GROUP * NSA_BRANCHES
    ngate = gate_logits.reshape(batch, seq, NSA_KV_HEADS, HEAD_DIM)[..., :per_kv].reshape(
        batch, seq, GROUP_HEADS, NSA_BRANCHES)
    y_ret = _x_retention(heads(COL_RQ, 8), heads(COL_RK, 8), heads(COL_RV, 8), flat(COL_RG, 8), ret_norm)
    y_moba = _x_moba(heads(COL_MQ, 8), heads(COL_MK, 8), heads(COL_MV, 8))
    y_lru = _x_lru(flat(COL_LX, 8), flat(COL_LG, 8), lru_conv_w, lru_conv_b, lru_wa, lru_ba, lru_wx, lru_bx,
                   lru_lambda)
    y_nsa = _x_nsa(heads(COL_NQ, 8), heads(COL_NKC, 2), heads(COL_NVC, 2), heads(COL_NKS, 2), heads(COL_NVS, 2),
                   heads(COL_NKW, 2), heads(COL_NVW, 2), ngate,
                   cmp_pos_k, cmp_w1_k, cmp_w2_k, cmp_pos_v, cmp_w1_v, cmp_w2_v)
    ys = tuple(_mx(y.reshape(batch * seq, GROUP_WIDTH)) for y in (y_ret, y_moba, y_lru, y_nsa))
    x = _outproj(ys, _mx(w_out), x)
    h = _rmsnorm(x, norm_ffn, MXU_DTYPE)
    u = _gateup(h, _mx(w_gate), _mx(w_up))
    return _down(u, _mx(w_down), x)


def kernel(x, norm_mix, w_in, w_out, ret_norm, lru_conv_w, lru_conv_b, lru_wa, lru_ba, lru_wx, lru_bx, lru_lambda, cmp_pos_k, cmp_w1_k, cmp_w2_k, cmp_pos_v, cmp_w1_v, cmp_w2_v, norm_ffn, w_gate, w_up, w_down, norm_final):
    batch, seq, d = x.shape
    x = x.reshape(batch * seq, d)
    for l in range(norm_mix.shape[0]):
        x = _layer(x, batch, seq, norm_mix[l], w_in[l], w_out[l], ret_norm[l], lru_conv_w[l], lru_conv_b[l],
                   lru_wa[l], lru_ba[l], lru_wx[l], lru_bx[l], lru_lambda[l],
                   cmp_pos_k[l], cmp_w1_k[l], cmp_w2_k[l], cmp_pos_v[l], cmp_w1_v[l], cmp_w2_v[l],
                   norm_ffn[l], w_gate[l], w_up[l], w_down[l])
    return _rmsnorm(x, norm_final, F32).reshape(batch, seq, d)
```
